```python
import math
import jax, jax.numpy as jnp
from jax import lax
import numpy as np

D_MODEL = 1024
BATCH = 4
SEQ = 4096
DEPTH = 4
DEC_BATCH = 128
DEC_SEQ = 4
PAST_LEN = 8192
PAGE_SIZE = 128

RET_HEADS = 4
RET_QK_DIM = D_MODEL // RET_HEADS
RET_V_DIM = 2 * RET_QK_DIM
RET_CHUNK = 128
SWA_HEAD_DIM = 64
SWA_Q_HEADS = D_MODEL // SWA_HEAD_DIM
SWA_KV_HEADS = 4
SWA_GROUP = SWA_Q_HEADS // SWA_KV_HEADS
WINDOW = 128
SWA_BLOCK = WINDOW
D_FF = 4 * D_MODEL
ROPE_THETA = 10000.0
EPS = 1e-6
NEG_INF = -1e30
N_RET = (DEPTH + 1) // 2
N_SWA = DEPTH // 2

kernel_name = "hybrid_retention_swa_sink_decoder_step"


def rms_norm(x, g):
    xf = x.astype(jnp.float32)
    y = xf * lax.rsqrt(jnp.mean(xf * xf, axis=-1, keepdims=True) + EPS)
    return (y * g.astype(jnp.float32)).astype(x.dtype)


def rope(x, pos):
    d = x.shape[-1]
    half = d // 2
    inv = ROPE_THETA ** (-jnp.arange(half, dtype=jnp.float32) / half)
    ang = pos.astype(jnp.float32)[:, None] * inv[None, :]
    cos = jnp.cos(ang)[:, None, :]
    sin = jnp.sin(ang)[:, None, :]
    xf = x.astype(jnp.float32)
    x1, x2 = xf[..., :half], xf[..., half:]
    return jnp.concatenate([x1 * cos - x2 * sin, x2 * cos + x1 * sin], axis=-1).astype(x.dtype)


def retention(h, s0, pos, w_in, w_out):
    B, T, _ = h.shape
    H, dk, dv = RET_HEADS, RET_QK_DIM, RET_V_DIM
    proj = (h @ w_in).astype(jnp.float32)
    q = proj[..., :H * dk].reshape(B, T, H, dk)
    k = proj[..., H * dk:2 * H * dk].reshape(B, T, H, dk)
    v = proj[..., 2 * H * dk:2 * H * dk + H * dv].reshape(B, T, H, dv)
    g = proj[..., 2 * H * dk + H * dv:]
    q = rope(q, pos)
    k = rope(k, pos) * (dk ** -0.5)
    C = RET_CHUNK if T % RET_CHUNK == 0 else T
    nc = T // C
    log_g = jnp.log1p(-jnp.exp2(-5.0 - jnp.arange(H, dtype=jnp.float32)))
    idx = jnp.arange(C, dtype=jnp.float32)
    diff = idx[:, None] - idx[None, :]
    dmat = jnp.where(diff >= 0, jnp.exp(jnp.maximum(diff, 0.0)[None] * log_g[:, None, None]), 0.0)
    q_dec = jnp.exp((idx + 1.0)[:, None] * log_g[None, :])
    k_dec = jnp.exp((C - 1.0 - idx)[:, None] * log_g[None, :])
    c_dec = jnp.exp(C * log_g)
    qc = q.reshape(B, nc, C, H, dk)
    kc = k.reshape(B, nc, C, H, dk)
    vc = v.reshape(B, nc, C, H, dv)
    scores = jnp.einsum('bnihd,bnjhd->bnhij', qc, kc) * dmat[None, None]
    intra = jnp.einsum('bnhij,bnjhe->bnihe', scores, vc)

    def step(S, xs):
        qi, ki, vi = xs
        cross = jnp.einsum('bihd,bhde->bihe', qi * q_dec[None, :, :, None], S)
        S = S * c_dec[None, :, None, None] + jnp.einsum('bjhd,bjhe->bhde', ki * k_dec[None, :, :, None], vi)
        return S, cross

    s_fin, cross = lax.scan(step, s0.astype(jnp.float32),
                            (qc.transpose(1, 0, 2, 3, 4), kc.transpose(1, 0, 2, 3, 4), vc.transpose(1, 0, 2, 3, 4)))
    o = (intra + cross.transpose(1, 0, 2, 3, 4)).reshape(B, T, H, dv)
    o = o * lax.rsqrt(jnp.mean(o * o, axis=-1, keepdims=True) + EPS)
    o = o.reshape(B, T, H * dv) * jax.nn.silu(g)
    return o.astype(h.dtype) @ w_out, s_fin


def sink_softmax(s, mask, sinks):
    s = jnp.where(mask, s, NEG_INF)
    sk = sinks.astype(jnp.float32)[:, :, None, None]
    m = jnp.maximum(jnp.max(s, axis=-1, keepdims=True), sk)
    p = jnp.exp(s - m)
    return p / (jnp.sum(p, axis=-1, keepdims=True) + jnp.exp(sk - m))


def swa_qkv(h, pos, w_qkv, q_norm, k_norm):
    B, T, _ = h.shape
    Hq, Hkv, dh = SWA_Q_HEADS, SWA_KV_HEADS, SWA_HEAD_DIM
    proj = h @ w_qkv
    q = proj[..., :Hq * dh].reshape(B, T, Hq, dh)
    k = proj[..., Hq * dh:(Hq + Hkv) * dh].reshape(B, T, Hkv, dh)
    v = proj[..., (Hq + Hkv) * dh:].reshape(B, T, Hkv, dh)
    q = rope(rms_norm(q, q_norm), pos)
    k = rope(rms_norm(k, k_norm), pos)
    return q.reshape(B, T, Hkv, SWA_GROUP, dh), k, v


def swa_prompt(h, pos, w_qkv, q_norm, k_norm, sinks, w_o):
    B, T, _ = h.shape
    q, k, v = swa_qkv(h, pos, w_qkv, q_norm, k_norm)
    nb = T // SWA_BLOCK
    qb = q.reshape(B, nb, SWA_BLOCK, SWA_KV_HEADS, SWA_GROUP, SWA_HEAD_DIM)
    kb = k.reshape(B, nb, SWA_BLOCK, SWA_KV_HEADS, SWA_HEAD_DIM)
    vb = v.reshape(B, nb, SWA_BLOCK, SWA_KV_HEADS, SWA_HEAD_DIM)
    shift = lambda a: jnp.concatenate([jnp.zeros_like(a[:, :1]), a[:, :-1]], axis=1)
    kk = jnp.concatenate([shift(kb), kb], axis=2)
    vv = jnp.concatenate([shift(vb), vb], axis=2)
    blk = jnp.arange(nb)[:, None]
    qpos = blk * SWA_BLOCK + jnp.arange(SWA_BLOCK)[None, :]
    kpos = (blk - 1) * SWA_BLOCK + jnp.arange(2 * SWA_BLOCK)[None, :]
    d = qpos[:, :, None] - kpos[:, None, :]
    mask = (d >= 0) & (d < WINDOW) & (kpos[:, None, :] >= 0)
    s = jnp.einsum('bnikgd,bnjkd->bnkgij', qb, kk).astype(jnp.float32) * (SWA_HEAD_DIM ** -0.5)
    p = sink_softmax(s, mask[None, :, None, None], sinks.reshape(SWA_KV_HEADS, SWA_GROUP))
    o = jnp.einsum('bnkgij,bnjkd->bnikgd', p.astype(vv.dtype), vv).reshape(B, T, SWA_Q_HEADS * SWA_HEAD_DIM)
    return o @ w_o, k[:, -WINDOW:], v[:, -WINDOW:]


def swa_sample(h, pos, cache_k, cache_v, w_qkv, q_norm, k_norm, sinks, w_o):
    B, T, _ = h.shape
    L = cache_k.shape[1]
    q, k, v = swa_qkv(h, pos, w_qkv, q_norm, k_norm)
    kk = jnp.concatenate([cache_k.astype(k.dtype), k], axis=1)
    vv = jnp.concatenate([cache_v.astype(v.dtype), v], axis=1)
    kpos = PAST_LEN - L + jnp.arange(L + T)
    d = pos[:, None] - kpos[None, :]
    mask = (d >= 0) & (d < WINDOW)
    s = jnp.einsum('btkgd,bskd->bkgts', q, kk).astype(jnp.float32) * (SWA_HEAD_DIM ** -0.5)
    p = sink_softmax(s, mask[None, None, None], sinks.reshape(SWA_KV_HEADS, SWA_GROUP))
    o = jnp.einsum('bkgts,bskd->btkgd', p.astype(vv.dtype), vv).reshape(B, T, SWA_Q_HEADS * SWA_HEAD_DIM)
    return o @ w_o, kk[:, -L:], vv[:, -L:]


def sq_relu_mlp(h, w_up, w_down):
    a = jax.nn.relu(h @ w_up)
    return (a * a) @ w_down


def setup_inputs(seed: int = 0) -> dict:
    key = jax.random.key(seed)
    ks = jax.random.split(key, 16)
    f32 = jnp.float32
    nrm = lambda k, shape, scale: jax.random.normal(k, shape, f32) * scale
    L = min(WINDOW, PAST_LEN)
    ret_in = 2 * RET_HEADS * RET_QK_DIM + 2 * RET_HEADS * RET_V_DIM
    swa_in = (SWA_Q_HEADS + 2 * SWA_KV_HEADS) * SWA_HEAD_DIM
    return {
        "x_prompt": nrm(ks[0], (BATCH, SEQ, D_MODEL), 1.0),
        "x_sample": nrm(ks[1], (DEC_BATCH, DEC_SEQ, D_MODEL), 1.0),
        "state_ret": nrm(ks[2], (N_RET, DEC_BATCH, RET_HEADS, RET_QK_DIM, RET_V_DIM), 0.5),
        "cache_swa_k": nrm(ks[3], (N_SWA, DEC_BATCH, L, SWA_KV_HEADS, SWA_HEAD_DIM), 1.0),
        "cache_swa_v": nrm(ks[4], (N_SWA, DEC_BATCH, L, SWA_KV_HEADS, SWA_HEAD_DIM), 1.0),
        "norm_mix": 1.0 + nrm(ks[5], (DEPTH, D_MODEL), 0.1),
        "norm_ffn": 1.0 + nrm(ks[6], (DEPTH, D_MODEL), 0.1),
        "ret_w_in": nrm(ks[7], (N_RET, D_MODEL, ret_in), D_MODEL ** -0.5),
        "ret_w_out": nrm(ks[8], (N_RET, RET_HEADS * RET_V_DIM, D_MODEL), (RET_HEADS * RET_V_DIM) ** -0.5),
        "swa_w_qkv": nrm(ks[9], (N_SWA, D_MODEL, swa_in), D_MODEL ** -0.5),
        "swa_q_norm": 1.0 + nrm(ks[10], (N_SWA, SWA_HEAD_DIM), 0.1),
        "swa_k_norm": 1.0 + nrm(ks[11], (N_SWA, SWA_HEAD_DIM), 0.1),
        "swa_sinks": nrm(ks[12], (N_SWA, SWA_Q_HEADS), 1.0),
        "swa_w_o": nrm(ks[13], (N_SWA, SWA_Q_HEADS * SWA_HEAD_DIM, D_MODEL), (SWA_Q_HEADS * SWA_HEAD_DIM) ** -0.5),
        "ffn_w_up": nrm(ks[14], (DEPTH, D_MODEL, D_FF), D_MODEL ** -0.5),
        "ffn_w_down": nrm(ks[15], (DEPTH, D_FF, D_MODEL), D_FF ** -0.5),
    }


def reference(x_prompt, x_sample, state_ret, cache_swa_k, cache_swa_v, norm_mix, norm_ffn,
              ret_w_in, ret_w_out, swa_w_qkv, swa_q_norm, swa_k_norm, swa_sinks, swa_w_o,
              ffn_w_up, ffn_w_down):
    pos_p = jnp.arange(SEQ)
    pos_s = PAST_LEN + jnp.arange(DEC_SEQ)
    xp, xs = x_prompt, x_sample
    ret_p, ret_s, kp, vp, ksmp, vsmp = [], [], [], [], [], []
    for i in range(DEPTH):
        hp = rms_norm(xp, norm_mix[i])
        hs = rms_norm(xs, norm_mix[i])
        if i % 2 == 0:
            r = i // 2
            s0 = jnp.zeros((BATCH, RET_HEADS, RET_QK_DIM, RET_V_DIM), jnp.float32)
            op, sp = retention(hp, s0, pos_p, ret_w_in[r], ret_w_out[r])
            os_, ss = retention(hs, state_ret[r], pos_s, ret_w_in[r], ret_w_out[r])
            ret_p.append(sp)
            ret_s.append(ss)
        else:
            a = i // 2
            op, k1, v1 = swa_prompt(hp, pos_p, swa_w_qkv[a], swa_q_norm[a], swa_k_norm[a], swa_sinks[a], swa_w_o[a])
            os_, k2, v2 = swa_sample(hs, pos_s, cache_swa_k[a], cache_swa_v[a], swa_w_qkv[a], swa_q_norm[a],
                                     swa_k_norm[a], swa_sinks[a], swa_w_o[a])
            kp.append(k1)
            vp.append(v1)
            ksmp.append(k2)
            vsmp.append(v2)
        xp = xp + op.astype(xp.dtype)
        xs = xs + os_.astype(xs.dtype)
        xp = xp + sq_relu_mlp(rms_norm(xp, norm_ffn[i]), ffn_w_up[i], ffn_w_down[i]).astype(xp.dtype)
        xs = xs + sq_relu_mlp(rms_norm(xs, norm_ffn[i]), ffn_w_up[i], ffn_w_down[i]).astype(xs.dtype)
    return (xp, xs, jnp.stack(ret_p), jnp.stack(kp), jnp.stack(vp), jnp.stack(ret_s), jnp.stack(ksmp), jnp.stack(vsmp))
```

```python
import functools

import jax
import jax.numpy as jnp
from jax import lax
from jax.experimental import pallas as pl
from jax.experimental.pallas import tpu as pltpu

PAST_LEN = 8192
WINDOW = 128
RET_CHUNK = 128
ROPE_THETA = 10000.0
EPS = 1e-6
NEG_INF = -1e30

V7X_LANES = 128
V7X_VMEM_LIMIT_BYTES = 56 * 1024 * 1024

TOKEN_TILE = 512
FFN_CHUNK = 1024
SAMPLE_RET_BATCH = 8
SAMPLE_SWA_BATCH = 4

BF16 = jnp.bfloat16
F32 = jnp.float32


def _params(semantics):
    return pltpu.CompilerParams(dimension_semantics=semantics, vmem_limit_bytes=V7X_VMEM_LIMIT_BYTES)


def _resident(shape):
    zeros = (0,) * len(shape)
    return pl.BlockSpec(shape, lambda *_: zeros, pipeline_mode=pl.Buffered(1))


def _smem():
    return pl.BlockSpec(memory_space=pltpu.SMEM)


def _rms_scale(x):
    return x * lax.rsqrt(jnp.mean(x * x, axis=-1, keepdims=True) + EPS)


def _dot(a, b):
    return jnp.dot(a, b, preferred_element_type=F32)


def _dot_nt(a, b):
    return lax.dot_general(a, b, (((1,), (1,)), ((), ())), preferred_element_type=F32)


def _dot_tn(a, b):
    return lax.dot_general(a, b, (((0,), (0,)), ((), ())), preferred_element_type=F32)


def _ret_proj_kernel(x_ref, gamma_ref, w_ref, cos_ref, sin_ref, q_ref, k_ref, v_ref, g_ref, *, qk_width, dk):
    h = (_rms_scale(x_ref[...]) * gamma_ref[...]).astype(BF16)
    cos = cos_ref[...]
    sin = sin_ref[...]
    half = dk // 2

    def rope_store(proj, out_ref, scale):
        for head in range(qk_width // dk):
            x1 = proj[:, head * dk:head * dk + half]
            x2 = proj[:, head * dk + half:(head + 1) * dk]
            out_ref[:, head * dk:head * dk + half] = ((x1 * cos - x2 * sin) * scale).astype(out_ref.dtype)
            out_ref[:, head * dk + half:(head + 1) * dk] = ((x2 * cos + x1 * sin) * scale).astype(out_ref.dtype)

    rope_store(_dot(h, w_ref[:, :qk_width]), q_ref, 1.0)
    rope_store(_dot(h, w_ref[:, qk_width:2 * qk_width]), k_ref, dk ** -0.5)
    v_width = v_ref.shape[1]
    v_ref[...] = _dot(h, w_ref[:, 2 * qk_width:2 * qk_width + v_width]).astype(v_ref.dtype)
    g_ref[...] = _dot(h, w_ref[:, 2 * qk_width + v_width:])


def _ret_proj(x, gamma, w, cos, sin, *, heads, dk, dv):
    n, d = x.shape
    tm = min(TOKEN_TILE, n)
    pos_blocks = cos.shape[0] // tm
    qk_width, v_width = heads * dk, heads * dv
    row = lambda i: (i, 0)
    pos = lambda i: (i % pos_blocks, 0)
    return pl.pallas_call(
        functools.partial(_ret_proj_kernel, qk_width=qk_width, dk=dk),
        grid=(n // tm,),
        in_specs=[pl.BlockSpec((tm, d), row), _resident((1, d)), _resident(w.shape),
                  pl.BlockSpec((tm, dk // 2), pos), pl.BlockSpec((tm, dk // 2), pos)],
        out_specs=[pl.BlockSpec((tm, qk_width), row), pl.BlockSpec((tm, qk_width), row),
                   pl.BlockSpec((tm, v_width), row), pl.BlockSpec((tm, v_width), row)],
        out_shape=[jax.ShapeDtypeStruct((n, qk_width), BF16), jax.ShapeDtypeStruct((n, qk_width), BF16),
                   jax.ShapeDtypeStruct((n, v_width), BF16), jax.ShapeDtypeStruct((n, v_width), F32)],
        compiler_params=_params(("parallel",)),
        name="ret_proj",
    )(x, gamma, w, cos, sin)


def _gated_group_norm(o, g):
    o = _rms_scale(o)
    return (o * (g * (1.0 / (1.0 + jnp.exp(-g))))).astype(BF16)


def _ret_prompt_kernel(cdec_ref, q_ref, k_ref, v_ref, g_ref, dmat_ref, qdec_ref, kdec_ref, o_ref, s_ref,
                       *, heads, dk, dv):
    @pl.when(pl.program_id(1) == 0)
    def _():
        s_ref[...] = jnp.zeros_like(s_ref)

    for head in range(heads):
        q = q_ref[:, head * dk:(head + 1) * dk]
        k = k_ref[:, head * dk:(head + 1) * dk]
        v = v_ref[:, head * dv:(head + 1) * dv]
        state = s_ref[0, head]
        scores = _dot_nt(q, k) * dmat_ref[head]
        intra = _dot(scores.astype(BF16), v)
        q_decayed = (q.astype(F32) * qdec_ref[head]).astype(BF16)
        cross = _dot(q_decayed, state.astype(BF16))
        k_decayed = (k.astype(F32) * kdec_ref[head]).astype(BF16)
        s_ref[0, head] = state * cdec_ref[head] + _dot_tn(k_decayed, v)
        o_ref[:, head * dv:(head + 1) * dv] = _gated_group_norm(intra + cross, g_ref[:, head * dv:(head + 1) * dv])


def _ret_prompt(q, k, v, g, decay, *, batch, heads, dk, dv):
    n = q.shape[0]
    c = RET_CHUNK
    nc = n // batch // c
    row = lambda b, j: (b * nc + j, 0)
    return pl.pallas_call(
        functools.partial(_ret_prompt_kernel, heads=heads, dk=dk, dv=dv),
        grid=(batch, nc),
        in_specs=[_smem(),
                  pl.BlockSpec((c, heads * dk), row), pl.BlockSpec((c, heads * dk), row),
                  pl.BlockSpec((c, heads * dv), row), pl.BlockSpec((c, heads * dv), row),
                  _resident((heads, c, c)), _resident((heads, c, dk)), _resident((heads, c, dk))],
        out_specs=[pl.BlockSpec((c, heads * dv), row),
                   pl.BlockSpec((1, heads, dk, dv), lambda b, j: (b, 0, 0, 0))],
        out_shape=[jax.ShapeDtypeStruct((n, heads * dv), BF16),
                   jax.ShapeDtypeStruct((batch, heads, dk, dv), F32)],
        compiler_params=_params(("parallel", "arbitrary")),
        name="ret_prompt",
    )(decay["cdec"], q, k, v, g, decay["dmat"], decay["qdec"], decay["kdec"])


def _ret_sample_kernel(cdec_ref, q_ref, k_ref, v_ref, g_ref, s_ref, dmat_ref, qdec_ref, kdec_ref, o_ref, snew_ref,
                       *, steps):
    head = pl.program_id(1)
    q = q_ref[...]
    k = k_ref[...]
    v = v_ref[...]
    rows = q.shape[0]
    scores = _dot_nt(q, k) * dmat_ref[0]
    out = _dot(scores.astype(BF16), v)
    q_decayed = (q.astype(F32) * qdec_ref[0]).astype(BF16)
    k_decayed = k.astype(F32) * kdec_ref[0]
    row_batch = lax.broadcasted_iota(jnp.int32, (rows, 1), 0) // steps
    for b in range(s_ref.shape[0]):
        mine = row_batch == b
        state = s_ref[b, 0]
        out = out + jnp.where(mine, _dot(q_decayed, state.astype(BF16)), 0.0)
        k_b = jnp.where(mine, k_decayed, 0.0).astype(BF16)
        snew_ref[b, 0] = state * cdec_ref[head] + _dot_tn(k_b, v)
    o_ref[...] = _gated_group_norm(out, g_ref[...])


def _ret_sample(q, k, v, g, state, decay, *, steps, heads, dk, dv):
    batch = state.shape[0]
    bb = SAMPLE_RET_BATCH
    rows = bb * steps
    qk = lambda i, h: (i, h)
    st = lambda i, h: (i, h, 0, 0)
    tab = lambda i, h: (h, 0, 0)
    return pl.pallas_call(
        functools.partial(_ret_sample_kernel, steps=steps),
        grid=(batch // bb, heads),
        in_specs=[_smem(),
                  pl.BlockSpec((rows, dk), qk), pl.BlockSpec((rows, dk), qk),
                  pl.BlockSpec((rows, dv), qk), pl.BlockSpec((rows, dv), qk),
                  pl.BlockSpec((bb, 1, dk, dv), st),
                  pl.BlockSpec((1, rows, rows), tab), pl.BlockSpec((1, rows, dk), tab),
                  pl.BlockSpec((1, rows, dk), tab)],
        out_specs=[pl.BlockSpec((rows, dv), qk), pl.BlockSpec((bb, 1, dk, dv), st)],
        out_shape=[jax.ShapeDtypeStruct((batch * steps, heads * dv), BF16),
                   jax.ShapeDtypeStruct(state.shape, F32)],
        compiler_params=_params(("parallel", "arbitrary")),
        name="ret_sample",
    )(decay["cdec"], q, k, v, g, state, decay["dmat"], decay["qdec"], decay["kdec"])


def _ret_decay_tables(heads, chunk, groups, dk):
    log_g = jnp.log1p(-jnp.exp2(-5.0 - jnp.arange(heads, dtype=F32)))
    idx = jnp.arange(chunk * groups)
    t = (idx % chunk).astype(F32)
    diff = t[:, None] - t[None, :]
    same = (idx[:, None] // chunk) == (idx[None, :] // chunk)
    dmat = jnp.where(same[None] & (diff >= 0)[None],
                     jnp.exp(jnp.maximum(diff, 0.0)[None] * log_g[:, None, None]), 0.0)
    qdec = jnp.exp((t + 1.0)[None, :] * log_g[:, None])
    kdec = jnp.exp((chunk - 1.0 - t)[None, :] * log_g[:, None])
    wide = lambda a: jnp.broadcast_to(a[:, :, None], a.shape + (dk,))
    return {"dmat": dmat, "qdec": wide(qdec), "kdec": wide(kdec), "cdec": jnp.exp(chunk * log_g)}


def _rope_tables(pos, dim):
    half = dim // 2
    inv = ROPE_THETA ** (-jnp.arange(half, dtype=F32) / half)
    ang = pos.astype(F32)[:, None] * inv[None, :]
    return jnp.cos(ang), jnp.sin(ang)


def _swa_proj_kernel(x_ref, gamma_ref, wq_ref, wkv_ref, ones_ref, qnorm_ref, knorm_ref, cos_ref, sin_ref,
                     q_ref, k_ref, v_ref, *, dh):
    h = (_rms_scale(x_ref[...]) * gamma_ref[...]).astype(BF16)
    cos = cos_ref[...]
    sin = sin_ref[...]
    low_half = (lax.broadcasted_iota(jnp.int32, (1, V7X_LANES), 1) % dh) < dh // 2

    def head_norm_rope(y, gain_ref, out_ref, scale):
        width = y.shape[1]
        sumsq = _dot((y * y).astype(BF16), ones_ref[:width, :width])
        y = y * lax.rsqrt(sumsq * (1.0 / dh) + EPS)
        for s in range(width // V7X_LANES):
            lanes = slice(s * V7X_LANES, (s + 1) * V7X_LANES)
            t = y[:, lanes] * gain_ref[...]
            partner = jnp.where(low_half, pltpu.roll(t, V7X_LANES - dh // 2, 1), pltpu.roll(t, dh // 2, 1))
            out_ref[:, lanes] = ((t * cos + partner * sin) * scale).astype(out_ref.dtype)

    head_norm_rope(_dot(h, wq_ref[...]), qnorm_ref, q_ref, dh ** -0.5)
    kv = _dot(h, wkv_ref[...])
    kv_width = k_ref.shape[1]
    head_norm_rope(kv[:, :kv_width], knorm_ref, k_ref, 1.0)
    v_ref[...] = kv[:, kv_width:]


def _swa_proj(x, gamma, wq, wkv, ones, qnorm, knorm, cos, sin, *, dh):
    n, d = x.shape
    tm = min(TOKEN_TILE, n)
    pos_blocks = cos.shape[0] // tm
    q_width, kv_width = wq.shape[1], wkv.shape[1] // 2
    row = lambda i: (i, 0)
    pos = lambda i: (i % pos_blocks, 0)
    return pl.pallas_call(
        functools.partial(_swa_proj_kernel, dh=dh),
        grid=(n // tm,),
        in_specs=[pl.BlockSpec((tm, d), row), _resident((1, d)), _resident(wq.shape), _resident(wkv.shape),
                  _resident(ones.shape), _resident((1, V7X_LANES)), _resident((1, V7X_LANES)),
                  pl.BlockSpec((tm, V7X_LANES), pos), pl.BlockSpec((tm, V7X_LANES), pos)],
        out_specs=[pl.BlockSpec((tm, q_width), row), pl.BlockSpec((tm, kv_width), row),
                   pl.BlockSpec((tm, kv_width), row)],
        out_shape=[jax.ShapeDtypeStruct((n, q_width), BF16), jax.ShapeDtypeStruct((n, kv_width), F32),
                   jax.ShapeDtypeStruct((n, kv_width), F32)],
        compiler_params=_params(("parallel",)),
        name="swa_proj",
    )(x, gamma, wq, wkv, ones, qnorm, knorm, cos, sin)


def _stack_heads(q, kv_heads, dh):
    kv_width = kv_heads * dh
    lane_head = lax.broadcasted_iota(jnp.int32, (1, kv_width), 1) // dh
    blocks = []
    for group in range(q.shape[1] // kv_width):
        q_group = q[:, group * kv_width:(group + 1) * kv_width]
        for head in range(kv_heads):
            blocks.append(jnp.where(lane_head == head, q_group, jnp.zeros_like(q_group)))
    return jnp.concatenate(blocks, axis=0)


def _sink_softmax(scores, valid, sink):
    s = jnp.where(valid, scores, NEG_INF)
    m = jnp.maximum(jnp.max(s, axis=-1, keepdims=True), sink)
    p = jnp.exp(s - m)
    denom = jnp.sum(p, axis=-1, keepdims=True) + jnp.exp(sink - m)
    return p * (1.0 / denom)


def _attend(q_stack, keys, values, valid, sinks_ref, rows, kv_heads, dh):
    kv_width = kv_heads * dh
    scores = _dot_nt(q_stack, keys)
    n_heads = q_stack.shape[0] // rows
    probs = [
        _sink_softmax(scores[i * rows:(i + 1) * rows], valid, sinks_ref[i]).astype(BF16) for i in range(n_heads)
    ]
    mixed = _dot(jnp.concatenate(probs, axis=0), values)
    lane_head = lax.broadcasted_iota(jnp.int32, (1, kv_width), 1) // dh
    outs = []
    for group in range(n_heads // kv_heads):
        acc = jnp.zeros((rows, kv_width), F32)
        for head in range(kv_heads):
            i = group * kv_heads + head
            acc = jnp.where(lane_head == head, mixed[i * rows:(i + 1) * rows], acc)
        outs.append(acc)
    return jnp.concatenate(outs, axis=1)


def _swa_prompt_kernel(sinks_ref, q_ref, kprev_ref, kcur_ref, vprev_ref, vcur_ref, o_ref, *, kv_heads, dh):
    blk = q_ref.shape[0]
    keys = jnp.concatenate([kprev_ref[...], kcur_ref[...]], axis=0).astype(BF16)
    values = jnp.concatenate([vprev_ref[...], vcur_ref[...]], axis=0).astype(BF16)
    qi = lax.broadcasted_iota(jnp.int32, (blk, 2 * blk), 0)
    kj = lax.broadcasted_iota(jnp.int32, (blk, 2 * blk), 1)
    dist = qi + blk - kj
    valid = (dist >= 0) & (dist < WINDOW) & ((kj >= blk) | (pl.program_id(1) > 0))
    q_stack = _stack_heads(q_ref[...], kv_heads, dh)
    o_ref[...] = _attend(q_stack, keys, values, valid, sinks_ref, blk, kv_heads, dh).astype(o_ref.dtype)


def _swa_prompt(q, k, v, sinks, *, batch, kv_heads, dh):
    n, q_width = q.shape
    blk = WINDOW
    nb = n // batch // blk
    kv_width = kv_heads * dh
    cur = lambda b, j: (b * nb + j, 0)
    prev = lambda b, j: (b * nb + jnp.maximum(j - 1, 0), 0)
    return pl.pallas_call(
        functools.partial(_swa_prompt_kernel, kv_heads=kv_heads, dh=dh),
        grid=(batch, nb),
        in_specs=[_smem(), pl.BlockSpec((blk, q_width), cur),
                  pl.BlockSpec((blk, kv_width), prev), pl.BlockSpec((blk, kv_width), cur),
                  pl.BlockSpec((blk, kv_width), prev), pl.BlockSpec((blk, kv_width), cur)],
        out_specs=pl.BlockSpec((blk, q_width), cur),
        out_shape=jax.ShapeDtypeStruct((n, q_width), BF16),
        compiler_params=_params(("parallel", "parallel")),
        name="swa_prompt",
    )(sinks, q, k, k, v, v)


def _swa_sample_kernel(sinks_ref, q_ref, knew_ref, vnew_ref, kcache_ref, vcache_ref, o_ref, *, steps, kv_heads, dh):
    rows = q_ref.shape[0]
    cache_len = kcache_ref.shape[1]
    n_keys = 2 * cache_len
    pad = jnp.zeros((n_keys - cache_len - rows, kv_heads * dh), BF16)
    knew = knew_ref[...].astype(BF16)
    vnew = vnew_ref[...].astype(BF16)
    q_stack = _stack_heads(q_ref[...], kv_heads, dh)
    qi = lax.broadcasted_iota(jnp.int32, (rows, n_keys), 0)
    kj = lax.broadcasted_iota(jnp.int32, (rows, n_keys), 1)
    q_batch, q_step = qi // steps, qi % steps
    new_row = kj - cache_len
    dist_cache = q_step + cache_len - kj
    cached_ok = (kj < cache_len) & (dist_cache >= 0) & (dist_cache < WINDOW)
    dist_new = q_step - new_row % steps
    new_ok = ((new_row >= 0) & (new_row < rows) & (new_row // steps == q_batch)
              & (dist_new >= 0) & (dist_new < WINDOW))
    out = jnp.zeros(o_ref.shape, F32)
    for b in range(kcache_ref.shape[0]):
        keys = jnp.concatenate([kcache_ref[b].astype(BF16), knew, pad], axis=0)
        values = jnp.concatenate([vcache_ref[b].astype(BF16), vnew, pad], axis=0)
        valid = (q_batch == b) & (cached_ok | new_ok)
        out = out + _attend(q_stack, keys, values, valid, sinks_ref, rows, kv_heads, dh)
    o_ref[...] = out.astype(o_ref.dtype)


def _swa_sample(q, k, v, kcache, vcache, sinks, *, steps, kv_heads, dh):
    n, q_width = q.shape
    batch, cache_len, kv_width = kcache.shape
    bb = SAMPLE_SWA_BATCH
    rows = bb * steps
    row = lambda i: (i, 0)
    cache = lambda i: (i, 0, 0)
    return pl.pallas_call(
        functools.partial(_swa_sample_kernel, steps=steps, kv_heads=kv_heads, dh=dh),
        grid=(batch // bb,),
        in_specs=[_smem(), pl.BlockSpec((rows, q_width), row),
                  pl.BlockSpec((rows, kv_width), row), pl.BlockSpec((rows, kv_width), row),
                  pl.BlockSpec((bb, cache_len, kv_width), cache), pl.BlockSpec((bb, cache_len, kv_width), cache)],
        out_specs=pl.BlockSpec((rows, q_width), row),
        out_shape=jax.ShapeDtypeStruct((n, q_width), BF16),
        compiler_params=_params(("parallel",)),
        name="swa_sample",
    )(sinks, q, k, v, kcache, vcache)


def _mix_ffn_kernel(x_ref, o_ref, wout_ref, gamma_ref, wup_ref, wdown_ref, y_ref):
    x = x_ref[...] + _dot(o_ref[...], wout_ref[...])
    h = (_rms_scale(x) * gamma_ref[...]).astype(BF16)
    acc = x
    for c in range(wup_ref.shape[1] // FFN_CHUNK):
        cols = slice(c * FFN_CHUNK, (c + 1) * FFN_CHUNK)
        a = jnp.maximum(_dot(h, wup_ref[:, cols]), 0.0)
        acc = acc + _dot((a * a).astype(BF16), wdown_ref[cols, :])
    y_ref[...] = acc


def _mix_ffn(x, o, wout, gamma, wup, wdown):
    n, d = x.shape
    tm = min(TOKEN_TILE, n)
    row = lambda i: (i, 0)
    return pl.pallas_call(
        _mix_ffn_kernel,
        grid=(n // tm,),
        in_specs=[pl.BlockSpec((tm, d), row), pl.BlockSpec((tm, o.shape[1]), row), _resident(wout.shape),
                  _resident((1, d)), _resident(wup.shape), _resident(wdown.shape)],
        out_specs=pl.BlockSpec((tm, d), row),
        out_shape=jax.ShapeDtypeStruct((n, d), F32),
        compiler_params=_params(("parallel",)),
        name="mix_ffn",
    )(x, o, wout, gamma, wup, wdown)


def kernel(x_prompt, x_sample, state_ret, cache_swa_k, cache_swa_v, norm_mix, norm_ffn, ret_w_in, ret_w_out,
           swa_w_qkv, swa_q_norm, swa_k_norm, swa_sinks, swa_w_o, ffn_w_up, ffn_w_down):
    batch, seq, d = x_prompt.shape
    dec_batch, dec_seq, _ = x_sample.shape
    _, _, ret_heads, dk, dv = state_ret.shape
    _, _, cache_len, kv_heads, dh = cache_swa_k.shape
    q_heads = swa_sinks.shape[1]
    groups = q_heads // kv_heads
    depth = norm_mix.shape[0]
    assert seq % TOKEN_TILE == 0 and seq % RET_CHUNK == 0 and seq % WINDOW == 0
    assert (dec_batch * dec_seq) % SAMPLE_RET_BATCH == 0 and dec_batch % SAMPLE_SWA_BATCH == 0
    assert cache_len == WINDOW and dec_seq % RET_CHUNK != 0

    xp = x_prompt.reshape(batch * seq, d)
    xs = x_sample.reshape(dec_batch * dec_seq, d)
    pos_p = jnp.arange(seq)
    pos_s = jnp.tile(PAST_LEN + jnp.arange(dec_seq), dec_batch)

    ret_rope_p = _rope_tables(pos_p, dk)
    ret_rope_s = _rope_tables(pos_s, dk)
    decay_p = _ret_decay_tables(ret_heads, RET_CHUNK, 1, dk)
    decay_s = _ret_decay_tables(ret_heads, dec_seq, SAMPLE_RET_BATCH, dk)

    def swa_rope(pos):
        cos, sin = _rope_tables(pos, dh)
        reps = V7X_LANES // dh
        cos = jnp.tile(jnp.concatenate([cos, cos], axis=1), (1, reps))
        sin = jnp.tile(jnp.concatenate([-sin, sin], axis=1), (1, reps))
        return cos, sin

    swa_rope_p = swa_rope(pos_p)
    swa_rope_s = swa_rope(pos_s)
    head_of = jnp.arange(q_heads * dh) // dh
    ones = (head_of[:, None] == head_of[None, :]).astype(BF16)
    lane_tile = lambda a: jnp.tile(a, V7X_LANES // dh)[None, :]

    ret_p, ret_s, kp, vp, ks, vs = [], [], [], [], [], []
    for i in range(depth):
        gamma = norm_mix[i][None, :]
        if i % 2 == 0:
            r = i // 2
            w_in = ret_w_in[r].astype(BF16)
            wout = ret_w_out[r].astype(BF16)
            q, k, v, g = _ret_proj(xp, gamma, w_in, *ret_rope_p, heads=ret_heads, dk=dk, dv=dv)
            op, state_p = _ret_prompt(q, k, v, g, decay_p, batch=batch, heads=ret_heads, dk=dk, dv=dv)
            q, k, v, g = _ret_proj(xs, gamma, w_in, *ret_rope_s, heads=ret_heads, dk=dk, dv=dv)
            os_, state_s = _ret_sample(q, k, v, g, state_ret[r], decay_s, steps=dec_seq, heads=ret_heads,
                                       dk=dk, dv=dv)
            ret_p.append(state_p)
            ret_s.append(state_s)
        else:
            a = i // 2
            w = swa_w_qkv[a]
            q_width = q_heads * dh
            wq = w[:, :q_width].reshape(d, kv_heads, groups, dh).transpose(0, 2, 1, 3).reshape(d, q_width)
            wq = wq.astype(BF16)
            wkv = w[:, q_width:].astype(BF16)
            wout = swa_w_o[a].reshape(kv_heads, groups, dh, d).transpose(1, 0, 2, 3).reshape(q_width, d)
            wout = wout.astype(BF16)
            sinks = swa_sinks[a].reshape(kv_heads, groups).T.reshape(q_heads)
            qn, kn = lane_tile(swa_q_norm[a]), lane_tile(swa_k_norm[a])
            q, k, v = _swa_proj(xp, gamma, wq, wkv, ones, qn, kn, *swa_rope_p, dh=dh)
            op = _swa_prompt(q, k, v, sinks, batch=batch, kv_heads=kv_heads, dh=dh)
            kp.append(k.reshape(batch, seq, kv_heads, dh)[:, -WINDOW:])
            vp.append(v.reshape(batch, seq, kv_heads, dh)[:, -WINDOW:])
            q, k, v = _swa_proj(xs, gamma, wq, wkv, ones, qn, kn, *swa_rope_s, dh=dh)
            kcache = cache_swa_k[a].reshape(dec_batch, cache_len, kv_heads * dh)
            vcache = cache_swa_v[a].reshape(dec_batch, cache_len, kv_heads * dh)
            os_ = _swa_sample(q, k, v, kcache, vcache, sinks, steps=dec_seq, kv_heads=kv_heads, dh=dh)
            k_all = jnp.concatenate([cache_swa_k[a], k.reshape(dec_batch, dec_seq, kv_heads, dh)], axis=1)
            v_all = jnp.concatenate([cache_swa_v[a], v.reshape(dec_batch, dec_seq, kv_heads, dh)], axis=1)
            ks.append(k_all[:, -cache_len:])
            vs.append(v_all[:, -cache_len:])
        gamma_ffn = norm_ffn[i][None, :]
        wup = ffn_w_up[i].astype(BF16)
        wdown = ffn_w_down[i].astype(BF16)
        xp = _mix_ffn(xp, op, wout, gamma_ffn, wup, wdown)
        xs = _mix_ffn(xs, os_, wout, gamma_ffn, wup, wdown)

    return (xp.reshape(batch, seq, d), xs.reshape(dec_batch, dec_seq, d), jnp.stack(ret_p), jnp.stack(kp),
            jnp.stack(vp), jnp.stack(ret_s), jnp.stack(ks), jnp.stack(vs))
```
